```python
import jax, jax.numpy as jnp
from jax import lax
import numpy as np

D_MODEL = 1024
BATCH = 32
SEQ = 2048
DEPTH = 4

MEM_LEN = 256
HEAD_DIM = D_MODEL // 16
A_HEADS = 6
B_HEADS = 4
C_HEADS = 6
D_A = A_HEADS * HEAD_DIM
D_B = B_HEADS * HEAD_DIM
D_C = C_HEADS * HEAD_DIM
CONV_A_WIDTH = 31
CONV_C_WIDTH = 3
CHUNK = 128
IN_WIDTH = 2 * D_A + 2 * D_B + 3 * D_C
SPLITS = (D_A, 2 * D_A, 2 * D_A + D_B, 2 * D_A + 2 * D_B,
          2 * D_A + 2 * D_B + D_C, 2 * D_A + 2 * D_B + 2 * D_C)
X_HEADS = 4
X_HEAD_DIM = D_MODEL // X_HEADS
D_FF = 4 * D_MODEL
LN_EPS = 1e-5
DEEPNORM_ALPHA = (2.0 * DEPTH) ** 0.25
DEEPNORM_BETA = (8.0 * DEPTH) ** -0.25

kernel_name = 'hybrid_conv_gmlp_shortconv_deepnorm_trunk'


def _layer_norm(x, g, b):
    xf = x.astype(jnp.float32)
    mu = jnp.mean(xf, axis=-1, keepdims=True)
    var = jnp.mean(jnp.square(xf - mu), axis=-1, keepdims=True)
    y = (xf - mu) * lax.rsqrt(var + LN_EPS)
    return (y * g.astype(jnp.float32) + b.astype(jnp.float32)).astype(x.dtype)


def _causal_depthwise_conv(x, w):
    k, c = w.shape
    return lax.conv_general_dilated(
        x, w[:, None, :].astype(x.dtype), window_strides=(1,), padding=[(k - 1, 0)],
        dimension_numbers=('NWC', 'WIO', 'NWC'), feature_group_count=c)


def _chunked_spatial_gate(u, v, w_s, b_s):
    bsz, s, _ = v.shape
    vh = v.reshape(bsz, s // CHUNK, CHUNK, B_HEADS, HEAD_DIM)
    causal = jnp.tril(jnp.ones((CHUNK, CHUNK), dtype=bool))
    w = jnp.where(causal[None], w_s, jnp.zeros_like(w_s)).astype(v.dtype)
    mixed = jnp.einsum('hts,bcshd->bcthd', w, vh) + b_s.T[None, None, :, :, None].astype(v.dtype)
    return u * mixed.reshape(bsz, s, D_B)


def _hybrid_mixer(h, w_in, conv_a_w, conv_a_b, ln_a_g, ln_a_b, ln_v_g, ln_v_b, w_s, b_s, conv_c_w, w_out):
    proj = h @ w_in
    a_val, a_gate, b_u, b_v, c_b, c_c, c_x = jnp.split(proj, SPLITS, axis=-1)
    a = _causal_depthwise_conv(a_val * jax.nn.sigmoid(a_gate), conv_a_w) + conv_a_b
    a = jax.nn.swish(_layer_norm(a, ln_a_g, ln_a_b))
    u = jax.nn.gelu(b_u, approximate=False)
    v = _layer_norm(jax.nn.gelu(b_v, approximate=False), ln_v_g, ln_v_b)
    bo = _chunked_spatial_gate(u, v, w_s, b_s)
    co = c_b * _causal_depthwise_conv(c_c * c_x, conv_c_w)
    return jnp.concatenate([a, bo, co], axis=-1) @ w_out


def _memory_cross_attention(h, mem, w_q, w_kv, w_o):
    bsz, s, _ = h.shape
    m = mem.shape[1]
    q = (h @ w_q).reshape(bsz, s, X_HEADS, X_HEAD_DIM)
    k, v = jnp.split(mem @ w_kv, 2, axis=-1)
    k = k.reshape(bsz, m, X_HEADS, X_HEAD_DIM)
    v = v.reshape(bsz, m, X_HEADS, X_HEAD_DIM)
    scores = jnp.einsum('bshd,bmhd->bhsm', q.astype(jnp.float32), k.astype(jnp.float32)) * (X_HEAD_DIM ** -0.5)
    p = jax.nn.softmax(scores, axis=-1).astype(h.dtype)
    o = jnp.einsum('bhsm,bmhd->bshd', p, v).reshape(bsz, s, D_MODEL)
    return o @ w_o


def _sq_relu_mlp(h, w_ff1, w_ff2):
    return jnp.square(jax.nn.relu(h @ w_ff1)) @ w_ff2


def setup_inputs(seed: int = 0) -> dict:
    key = jax.random.key(seed)
    ks = jax.random.split(key, 26)

    def nrm(k, shape, scale):
        return jax.random.normal(k, shape, dtype=jnp.float32) * scale

    def gain(k, n):
        return 1.0 + nrm(k, (DEPTH, n), 0.02)

    w_kv = jnp.concatenate([nrm(ks[16], (DEPTH, D_MODEL, D_MODEL), D_MODEL ** -0.5),
                            nrm(ks[17], (DEPTH, D_MODEL, D_MODEL), D_MODEL ** -0.5 * DEEPNORM_BETA)], axis=-1)
    return {
        'x': nrm(ks[0], (BATCH, SEQ, D_MODEL), 1.0),
        'mem': nrm(ks[1], (BATCH, MEM_LEN, D_MODEL), 1.0),
        'w_in': nrm(ks[2], (DEPTH, D_MODEL, IN_WIDTH), D_MODEL ** -0.5),
        'conv_a_w': nrm(ks[3], (DEPTH, CONV_A_WIDTH, D_A), CONV_A_WIDTH ** -0.5),
        'conv_a_b': nrm(ks[4], (DEPTH, D_A), 0.02),
        'ln_a_g': gain(ks[5], D_A),
        'ln_a_b': nrm(ks[6], (DEPTH, D_A), 0.02),
        'ln_v_g': gain(ks[7], D_B),
        'ln_v_b': nrm(ks[8], (DEPTH, D_B), 0.02),
        'w_s': nrm(ks[9], (DEPTH, B_HEADS, CHUNK, CHUNK), CHUNK ** -0.5),
        'b_s': 1.0 + nrm(ks[10], (DEPTH, B_HEADS, CHUNK), 0.02),
        'conv_c_w': nrm(ks[11], (DEPTH, CONV_C_WIDTH, D_C), CONV_C_WIDTH ** -0.5),
        'w_out': nrm(ks[12], (DEPTH, D_MODEL, D_MODEL), D_MODEL ** -0.5 * DEEPNORM_BETA),
        'ln1_g': gain(ks[13], D_MODEL),
        'ln1_b': nrm(ks[14], (DEPTH, D_MODEL), 0.02),
        'w_q': nrm(ks[15], (DEPTH, D_MODEL, D_MODEL), D_MODEL ** -0.5),
        'w_kv': w_kv,
        'w_o': nrm(ks[18], (DEPTH, D_MODEL, D_MODEL), D_MODEL ** -0.5 * DEEPNORM_BETA),
        'ln2_g': gain(ks[19], D_MODEL),
        'ln2_b': nrm(ks[20], (DEPTH, D_MODEL), 0.02),
        'w_ff1': nrm(ks[21], (DEPTH, D_MODEL, D_FF), D_MODEL ** -0.5),
        'w_ff2': nrm(ks[22], (DEPTH, D_FF, D_MODEL), D_FF ** -0.5 * DEEPNORM_BETA),
        'ln3_g': gain(ks[23], D_MODEL),
        'ln3_b': nrm(ks[24], (DEPTH, D_MODEL), 0.02),
    }


def reference(x, mem, w_in, conv_a_w, conv_a_b, ln_a_g, ln_a_b, ln_v_g, ln_v_b, w_s, b_s, conv_c_w, w_out,
              ln1_g, ln1_b, w_q, w_kv, w_o, ln2_g, ln2_b, w_ff1, w_ff2, ln3_g, ln3_b):
    for l in range(DEPTH):
        mix = _hybrid_mixer(x, w_in[l], conv_a_w[l], conv_a_b[l], ln_a_g[l], ln_a_b[l],
                            ln_v_g[l], ln_v_b[l], w_s[l], b_s[l], conv_c_w[l], w_out[l])
        x = _layer_norm(DEEPNORM_ALPHA * x + mix, ln1_g[l], ln1_b[l])
        att = _memory_cross_attention(x, mem, w_q[l], w_kv[l], w_o[l])
        x = _layer_norm(DEEPNORM_ALPHA * x + att, ln2_g[l], ln2_b[l])
        ff = _sq_relu_mlp(x, w_ff1[l], w_ff2[l])
        x = _layer_norm(DEEPNORM_ALPHA * x + ff, ln3_g[l], ln3_b[l])
    return x
```

```python
import functools

import jax
import jax.numpy as jnp
from jax import lax
from jax.experimental import pallas as pl
from jax.experimental.pallas import tpu as pltpu

D_MODEL = 1024
MEM_LEN = 256
HEAD_DIM = D_MODEL // 16
B_HEADS = 4
D_A = 6 * HEAD_DIM
D_B = B_HEADS * HEAD_DIM
D_C = 6 * HEAD_DIM
CONV_A_WIDTH = 31
CONV_C_WIDTH = 3
CHUNK = 128
IN_WIDTH = 2 * D_A + 2 * D_B + 3 * D_C
X_HEADS = 4
X_HEAD_DIM = D_MODEL // X_HEADS
D_FF = 4 * D_MODEL
LN_EPS = 1e-5

SUBLANES = 8
LANES = 128
VMEM_LIMIT_BYTES = 56 * 1024 * 1024

ROW_TILE = 512
HALO_A = 32
HALO_C = SUBLANES
CONV_ROWS = 64
LN_ROWS = 32
KV_ROW_TILE = 1024

BF16 = jnp.bfloat16
F32 = jnp.float32


def _layer_norm(y, g, b):
    mu = jnp.mean(y, axis=-1, keepdims=True)
    d = y - mu
    var = jnp.mean(d * d, axis=-1, keepdims=True)
    return d * lax.rsqrt(var + LN_EPS) * g + b


def _gelu(x):
    return 0.5 * x * (1.0 + lax.erf(x * (0.5 ** 0.5)))


def _residual_ln(alpha, x_ref, y_buf, g_ref, b_ref, o_ref, rows):
    g = g_ref[...]
    b = b_ref[...]
    for r in range(0, rows, LN_ROWS):
        y = alpha * x_ref[r:r + LN_ROWS, :] + y_buf[r:r + LN_ROWS, :]
        o_ref[r:r + LN_ROWS, :] = _layer_norm(y, g, b)


def _mixer_kernel(alpha, x_ref, w_in_ref, caw_ref, cab_ref, lnag_ref, lnab_ref,
                  lnvg_ref, lnvb_ref, ws_ref, bs_ref, ccw_ref, w_out_ref,
                  g1_ref, b1_ref, o_ref,
                  proj_buf, ga_buf, gc_buf, cat_buf, mix_buf):
    tm = ROW_TILE
    j = pl.program_id(1)

    @pl.when(j == 0)
    def _():
        ga_buf[0:HALO_A, :] = jnp.zeros((HALO_A, D_A), F32)
        gc_buf[0:HALO_C, :] = jnp.zeros((HALO_C, D_C), F32)

    @pl.when(j > 0)
    def _():
        ga_buf[0:HALO_A, :] = ga_buf[tm:tm + HALO_A, :]
        gc_buf[0:HALO_C, :] = gc_buf[tm:tm + HALO_C, :]

    xb = x_ref[...].astype(BF16)
    proj_buf[...] = jnp.dot(xb, w_in_ref[...], preferred_element_type=F32)

    o_av, o_ag = 0, D_A
    o_bu, o_bv = 2 * D_A, 2 * D_A + D_B
    o_cb = 2 * D_A + 2 * D_B
    o_cc, o_cx = o_cb + D_C, o_cb + 2 * D_C

    for r in range(0, tm, CONV_ROWS):
        val = proj_buf[r:r + CONV_ROWS, o_av:o_av + D_A]
        gate = proj_buf[r:r + CONV_ROWS, o_ag:o_ag + D_A]
        ga_buf[HALO_A + r:HALO_A + r + CONV_ROWS, :] = val * jax.nn.sigmoid(gate)
    cab = cab_ref[...]
    lnag = lnag_ref[...]
    lnab = lnab_ref[...]
    for r in range(0, tm, CONV_ROWS):
        base = HALO_A + r - (CONV_A_WIDTH - 1)
        acc = caw_ref[0:1, :] * ga_buf[base:base + CONV_ROWS, :]
        for k in range(1, CONV_A_WIDTH):
            acc = acc + caw_ref[k:k + 1, :] * ga_buf[base + k:base + k + CONV_ROWS, :]
        a = _layer_norm(acc + cab, lnag, lnab)
        a = a * jax.nn.sigmoid(a)
        cat_buf[r:r + CONV_ROWS, 0:D_A] = a.astype(BF16)

    nrow = B_HEADS * CHUNK
    t_idx = lax.broadcasted_iota(jnp.int32, (nrow, CHUNK), 0) % CHUNK
    s_idx = lax.broadcasted_iota(jnp.int32, (nrow, CHUNK), 1)
    w_tril = jnp.where(s_idx <= t_idx, ws_ref[...], 0.0).astype(BF16)
    bs_col = bs_ref[...]
    lane_head = lax.broadcasted_iota(jnp.int32, (CHUNK, D_B), 1) // HEAD_DIM
    lnvg = lnvg_ref[...]
    lnvb = lnvb_ref[...]
    for c in range(0, tm, CHUNK):
        u = _gelu(proj_buf[c:c + CHUNK, o_bu:o_bu + D_B])
        v = _layer_norm(_gelu(proj_buf[c:c + CHUNK, o_bv:o_bv + D_B]), lnvg, lnvb)
        prod = jnp.dot(w_tril, v.astype(BF16), preferred_element_type=F32) + bs_col
        mixed = prod[0:CHUNK, :]
        for h in range(1, B_HEADS):
            mixed = jnp.where(lane_head == h, prod[h * CHUNK:(h + 1) * CHUNK, :], mixed)
        cat_buf[c:c + CHUNK, D_A:D_A + D_B] = (u * mixed).astype(BF16)

    for r in range(0, tm, CONV_ROWS):
        cc = proj_buf[r:r + CONV_ROWS, o_cc:o_cc + D_C]
        cx = proj_buf[r:r + CONV_ROWS, o_cx:o_cx + D_C]
        gc_buf[HALO_C + r:HALO_C + r + CONV_ROWS, :] = cc * cx
    for r in range(0, tm, CONV_ROWS):
        base = HALO_C + r - (CONV_C_WIDTH - 1)
        acc = ccw_ref[0:1, :] * gc_buf[base:base + CONV_ROWS, :]
        for k in range(1, CONV_C_WIDTH):
            acc = acc + ccw_ref[k:k + 1, :] * gc_buf[base + k:base + k + CONV_ROWS, :]
        cb = proj_buf[r:r + CONV_ROWS, o_cb:o_cb + D_C]
        cat_buf[r:r + CONV_ROWS, D_A + D_B:D_MODEL] = (cb * acc).astype(BF16)

    mix_buf[...] = jnp.dot(cat_buf[...], w_out_ref[...], preferred_element_type=F32)
    _residual_ln(alpha, x_ref, mix_buf, g1_ref, b1_ref, o_ref, tm)


def _attn_kernel(alpha, x_ref, wq_ref, k_ref, v_ref, wo_ref, g2_ref, b2_ref, o_ref,
                 q_buf, s_buf, p_buf, o_buf, att_buf):
    tm = ROW_TILE
    scale = X_HEAD_DIM ** -0.5
    xb = x_ref[...].astype(BF16)
    q_buf[...] = jnp.dot(xb, wq_ref[...], preferred_element_type=F32).astype(BF16)
    for h in range(X_HEADS):
        lo = h * X_HEAD_DIM
        s_buf[...] = lax.dot_general(
            q_buf[:, lo:lo + X_HEAD_DIM], k_ref[:, lo:lo + X_HEAD_DIM],
            (((1,), (1,)), ((), ())), preferred_element_type=F32)
        for r in range(0, tm, CONV_ROWS):
            s = s_buf[r:r + CONV_ROWS, :] * scale
            e = jnp.exp(s - jnp.max(s, axis=-1, keepdims=True))
            p = e / jnp.sum(e, axis=-1, keepdims=True)
            p_buf[r:r + CONV_ROWS, :] = p.astype(BF16)
        o_buf[:, lo:lo + X_HEAD_DIM] = jnp.dot(
            p_buf[...], v_ref[:, lo:lo + X_HEAD_DIM],
            preferred_element_type=F32).astype(BF16)
    att_buf[...] = jnp.dot(o_buf[...], wo_ref[...], preferred_element_type=F32)
    _residual_ln(alpha, x_ref, att_buf, g2_ref, b2_ref, o_ref, tm)


def _mlp_kernel(alpha, x_ref, w1_ref, w2_ref, g3_ref, b3_ref, o_ref, ff_buf):
    tm = ROW_TILE
    xb = x_ref[...].astype(BF16)
    acc = None
    for c in range(0, D_FF, D_MODEL):
        h = jnp.dot(xb, w1_ref[:, c:c + D_MODEL], preferred_element_type=F32)
        a = jnp.square(jnp.maximum(h, 0.0)).astype(BF16)
        part = jnp.dot(a, w2_ref[c:c + D_MODEL, :], preferred_element_type=F32)
        acc = part if acc is None else acc + part
    ff_buf[...] = acc
    _residual_ln(alpha, x_ref, ff_buf, g3_ref, b3_ref, o_ref, tm)


def _kv_kernel(mem_ref, wkv_ref, k_ref, v_ref):
    kv = jnp.dot(mem_ref[...].astype(BF16), wkv_ref[...], preferred_element_type=F32)
    k_ref[...] = kv[:, :D_MODEL].astype(BF16)
    v_ref[...] = kv[:, D_MODEL:].astype(BF16)


def _resident(shape, index_map):
    return pl.BlockSpec(shape, index_map, pipeline_mode=pl.Buffered(1))


def _params(n_axes):
    return pltpu.CompilerParams(
        dimension_semantics=("arbitrary",) * n_axes,
        vmem_limit_bytes=VMEM_LIMIT_BYTES)


def kernel(x, mem, w_in, conv_a_w, conv_a_b, ln_a_g, ln_a_b, ln_v_g, ln_v_b, w_s, b_s,
           conv_c_w, w_out, ln1_g, ln1_b, w_q, w_kv, w_o, ln2_g, ln2_b, w_ff1, w_ff2,
           ln3_g, ln3_b):
    batch, seq, d = x.shape
    depth = w_in.shape[0]
    mem_len = mem.shape[1]
    assert d == D_MODEL and mem_len == MEM_LEN and seq % ROW_TILE == 0
    alpha = (2.0 * depth) ** 0.25
    tm = ROW_TILE
    n_seq_tiles = seq // tm

    w_in_b, w_out_b, w_q_b, w_kv_b, w_o_b, w_ff1_b, w_ff2_b = (
        w.astype(BF16) for w in (w_in, w_out, w_q, w_kv, w_o, w_ff1, w_ff2))
    row = lambda p: p.reshape(depth, 1, p.shape[-1])
    conv_a_b, ln_a_g, ln_a_b, ln_v_g, ln_v_b = map(row, (conv_a_b, ln_a_g, ln_a_b, ln_v_g, ln_v_b))
    ln1_g, ln1_b, ln2_g, ln2_b, ln3_g, ln3_b = map(row, (ln1_g, ln1_b, ln2_g, ln2_b, ln3_g, ln3_b))
    w_s2 = w_s.reshape(depth, B_HEADS * CHUNK, CHUNK)
    b_s2 = b_s.reshape(depth, B_HEADS * CHUNK, 1)

    mem2 = mem.reshape(batch * mem_len, d)
    n_kv_tiles = (batch * mem_len) // KV_ROW_TILE
    k_all, v_all = pl.pallas_call(
        _kv_kernel,
        grid=(depth, n_kv_tiles),
        in_specs=[
            pl.BlockSpec((KV_ROW_TILE, d), lambda l, i: (i, 0)),
            pl.BlockSpec((None, d, 2 * d), lambda l, i: (l, 0, 0)),
        ],
        out_specs=[
            pl.BlockSpec((None, KV_ROW_TILE, d), lambda l, i: (l, i, 0)),
            pl.BlockSpec((None, KV_ROW_TILE, d), lambda l, i: (l, i, 0)),
        ],
        out_shape=[jax.ShapeDtypeStruct((depth, batch * mem_len, d), BF16)] * 2,
        compiler_params=_params(2),
        name="kv_proj",
    )(mem2, w_kv_b)
    k_all = k_all.reshape(depth, batch, mem_len, d)
    v_all = v_all.reshape(depth, batch, mem_len, d)

    x_spec = pl.BlockSpec((None, tm, d), lambda b, j: (b, j, 0))
    x_shape = jax.ShapeDtypeStruct((batch, seq, d), F32)

    for l in range(depth):
        lsel = lambda b, j, l=l: (l, 0, 0)
        vec = lambda n: _resident((None, 1, n), lsel)

        x = pl.pallas_call(
            functools.partial(_mixer_kernel, alpha),
            grid=(batch, n_seq_tiles),
            in_specs=[
                x_spec,
                _resident((None, d, IN_WIDTH), lsel),
                _resident((None, CONV_A_WIDTH, D_A), lsel),
                vec(D_A), vec(D_A), vec(D_A), vec(D_B), vec(D_B),
                _resident((None, B_HEADS * CHUNK, CHUNK), lsel),
                _resident((None, B_HEADS * CHUNK, 1), lsel),
                _resident((None, CONV_C_WIDTH, D_C), lsel),
                _resident((None, d, d), lsel),
                vec(d), vec(d),
            ],
            out_specs=x_spec,
            out_shape=x_shape,
            scratch_shapes=[
                pltpu.VMEM((tm, IN_WIDTH), F32),
                pltpu.VMEM((HALO_A + tm, D_A), F32),
                pltpu.VMEM((HALO_C + tm, D_C), F32),
                pltpu.VMEM((tm, d), BF16),
                pltpu.VMEM((tm, d), F32),
            ],
            compiler_params=_params(2),
            name="mixer",
        )(x, w_in_b, conv_a_w, conv_a_b, ln_a_g, ln_a_b, ln_v_g, ln_v_b, w_s2, b_s2,
          conv_c_w, w_out_b, ln1_g, ln1_b)

        x = pl.pallas_call(
            functools.partial(_attn_kernel, alpha),
            grid=(batch, n_seq_tiles),
            in_specs=[
                x_spec,
                _resident((None, d, d), lsel),
                pl.BlockSpec((None, None, mem_len, d), lambda b, j, l=l: (l, b, 0, 0)),
                pl.BlockSpec((None, None, mem_len, d), lambda b, j, l=l: (l, b, 0, 0)),
                _resident((None, d, d), lsel),
                vec(d), vec(d),
            ],
            out_specs=x_spec,
            out_shape=x_shape,
            scratch_shapes=[
                pltpu.VMEM((tm, d), BF16),
                pltpu.VMEM((tm, mem_len), F32),
                pltpu.VMEM((tm, mem_len), BF16),
                pltpu.VMEM((tm, d), BF16),
                pltpu.VMEM((tm, d), F32),
            ],
            compiler_params=_params(2),
            name="cross_attn",
        )(x, w_q_b, k_all, v_all, w_o_b, ln2_g, ln2_b)

        x = pl.pallas_call(
            functools.partial(_mlp_kernel, alpha),
            grid=(batch, n_seq_tiles),
            in_specs=[
                x_spec,
                _resident((None, d, D_FF), lsel),
                _resident((None, D_FF, d), lsel),
                vec(d), vec(d),
            ],
            out_specs=x_spec,
            out_shape=x_shape,
            scratch_shapes=[pltpu.VMEM((tm, d), F32)],
            compiler_params=_params(2),
            name="sq_relu_mlp",
        )(x, w_ff1_b, w_ff2_b, ln3_g, ln3_b)

    return x
```

```python
import functools

import jax
import jax.numpy as jnp
from jax import lax
from jax.experimental import pallas as pl
from jax.experimental.pallas import tpu as pltpu

D_MODEL = 1024
MEM_LEN = 256
HEAD_DIM = D_MODEL // 16
B_HEADS = 4
D_A = 6 * HEAD_DIM
D_B = B_HEADS * HEAD_DIM
D_C = 6 * HEAD_DIM
CONV_A_WIDTH = 31
CONV_C_WIDTH = 3
CHUNK = 128
IN_WIDTH = 2 * D_A + 2 * D_B + 3 * D_C
X_HEADS = 4
X_HEAD_DIM = D_MODEL // X_HEADS
D_FF = 4 * D_MODEL
LN_EPS = 1e-5

SUBLANES = 8
LANES = 128
VMEM_LIMIT_BYTES = 56 * 1024 * 1024

ROW_TILE = 512
TILES_PER_STEP = 2
STEP_ROWS = ROW_TILE * TILES_PER_STEP
HALO_A = 32
HALO_C = SUBLANES
CONV_ROWS = 64
LN_ROWS = 32
KV_ROW_TILE = 1024

BF16 = jnp.bfloat16
F32 = jnp.float32


def _layer_norm(y, g, b):
    mu = jnp.mean(y, axis=-1, keepdims=True)
    d = y - mu
    var = jnp.mean(d * d, axis=-1, keepdims=True)
    return d * lax.rsqrt(var + LN_EPS) * g + b


def _gelu(x):
    return 0.5 * x * (1.0 + lax.erf(x * (0.5 ** 0.5)))


def _causal_conv_rows(buf, w_ref, width, row0, rows):
    outs = []
    for cb in range(buf.shape[0]):
        lanes = slice(cb * LANES, (cb + 1) * LANES)
        base = row0 - (width - 1)
        acc = w_ref[0:1, lanes] * buf[cb, base:base + rows, :]
        for k in range(1, width):
            acc = acc + w_ref[k:k + 1, lanes] * buf[cb, base + k:base + k + rows, :]
        outs.append(acc)
    return jnp.concatenate(outs, axis=-1)


def _residual_ln(alpha, x_ref, y_buf, g_ref, b_ref, o_ref, row0, rows):
    g = g_ref[...]
    b = b_ref[...]
    for r in range(0, rows, LN_ROWS):
        y = alpha * x_ref[row0 + r:row0 + r + LN_ROWS, :] + y_buf[r:r + LN_ROWS, :]
        o_ref[row0 + r:row0 + r + LN_ROWS, :] = _layer_norm(y, g, b)


def _mixer_tile(alpha, proj_buf, row0, x_ref, caw_ref, cab_ref, lnag_ref, lnab_ref,
                lnvg_ref, lnvb_ref, ws_ref, bs_ref, ccw_ref, w_out_ref, g1_ref, b1_ref,
                o_ref, ga_buf, gc_buf, cat_buf, mix_buf):
    tm = ROW_TILE
    o_av, o_ag = 0, D_A
    o_bu, o_bv = 2 * D_A, 2 * D_A + D_B
    o_cb = 2 * D_A + 2 * D_B
    o_cc, o_cx = o_cb + D_C, o_cb + 2 * D_C

    for r in range(0, tm, CONV_ROWS):
        val = proj_buf[r:r + CONV_ROWS, o_av:o_av + D_A]
        gate = proj_buf[r:r + CONV_ROWS, o_ag:o_ag + D_A]
        g = val * jax.nn.sigmoid(gate)
        for cb in range(D_A // LANES):
            dst = HALO_A + row0 + r
            ga_buf[cb, dst:dst + CONV_ROWS, :] = g[:, cb * LANES:(cb + 1) * LANES]
    cab = cab_ref[...]
    lnag = lnag_ref[...]
    lnab = lnab_ref[...]
    for r in range(0, tm, CONV_ROWS):
        acc = _causal_conv_rows(ga_buf, caw_ref, CONV_A_WIDTH, HALO_A + row0 + r, CONV_ROWS)
        a = _layer_norm(acc + cab, lnag, lnab)
        a = a * jax.nn.sigmoid(a)
        cat_buf[r:r + CONV_ROWS, 0:D_A] = a.astype(BF16)

    nrow = B_HEADS * CHUNK
    t_idx = lax.broadcasted_iota(jnp.int32, (nrow, CHUNK), 0) % CHUNK
    s_idx = lax.broadcasted_iota(jnp.int32, (nrow, CHUNK), 1)
    w_tril = jnp.where(s_idx <= t_idx, ws_ref[...], 0.0).astype(BF16)
    bs_col = bs_ref[...]
    lane_head = lax.broadcasted_iota(jnp.int32, (CHUNK, D_B), 1) // HEAD_DIM
    lnvg = lnvg_ref[...]
    lnvb = lnvb_ref[...]
    for c in range(0, tm, CHUNK):
        u = _gelu(proj_buf[c:c + CHUNK, o_bu:o_bu + D_B])
        v = _layer_norm(_gelu(proj_buf[c:c + CHUNK, o_bv:o_bv + D_B]), lnvg, lnvb)
        prod = jnp.dot(w_tril, v.astype(BF16), preferred_element_type=F32) + bs_col
        mixed = prod[0:CHUNK, :]
        for h in range(1, B_HEADS):
            mixed = jnp.where(lane_head == h, prod[h * CHUNK:(h + 1) * CHUNK, :], mixed)
        cat_buf[c:c + CHUNK, D_A:D_A + D_B] = (u * mixed).astype(BF16)

    for r in range(0, tm, CONV_ROWS):
        cc = proj_buf[r:r + CONV_ROWS, o_cc:o_cc + D_C]
        cx = proj_buf[r:r + CONV_ROWS, o_cx:o_cx + D_C]
        g = cc * cx
        for cb in range(D_C // LANES):
            dst = HALO_C + row0 + r
            gc_buf[cb, dst:dst + CONV_ROWS, :] = g[:, cb * LANES:(cb + 1) * LANES]
    for r in range(0, tm, CONV_ROWS):
        acc = _causal_conv_rows(gc_buf, ccw_ref, CONV_C_WIDTH, HALO_C + row0 + r, CONV_ROWS)
        cb = proj_buf[r:r + CONV_ROWS, o_cb:o_cb + D_C]
        cat_buf[r:r + CONV_ROWS, D_A + D_B:D_MODEL] = (cb * acc).astype(BF16)

    mix_buf[...] = jnp.dot(cat_buf[...], w_out_ref[...], preferred_element_type=F32)
    _residual_ln(alpha, x_ref, mix_buf, g1_ref, b1_ref, o_ref, row0, tm)


def _mixer_kernel(alpha, tiles_per_seq, x_ref, w_in_ref, caw_ref, cab_ref,
                  lnag_ref, lnab_ref, lnvg_ref, lnvb_ref, ws_ref, bs_ref, ccw_ref,
                  w_out_ref, g1_ref, b1_ref, o_ref,
                  proj_buf, ga_buf, gc_buf, cat_buf, mix_buf):
    tm = ROW_TILE
    i = pl.program_id(0)

    seq_start = i % tiles_per_seq == 0

    @pl.when(seq_start)
    def _():
        ga_buf[:, 0:HALO_A, :] = jnp.zeros((D_A // LANES, HALO_A, LANES), F32)
        gc_buf[:, 0:HALO_C, :] = jnp.zeros((D_C // LANES, HALO_C, LANES), F32)

    @pl.when(jnp.logical_not(seq_start))
    def _():
        ga_buf[:, 0:HALO_A, :] = ga_buf[:, tm:tm + HALO_A, :]
        gc_buf[:, 0:HALO_C, :] = gc_buf[:, tm:tm + HALO_C, :]

    proj_buf[...] = jnp.dot(x_ref[...].astype(BF16), w_in_ref[...],
                            preferred_element_type=F32)
    _mixer_tile(alpha, proj_buf, 0, x_ref, caw_ref, cab_ref, lnag_ref, lnab_ref,
                lnvg_ref, lnvb_ref, ws_ref, bs_ref, ccw_ref, w_out_ref, g1_ref, b1_ref,
                o_ref, ga_buf, gc_buf, cat_buf, mix_buf)


def _attn_tile(alpha, q_buf, row0, x_ref, k_ref, v_ref, wo_ref, g2_ref, b2_ref, o_ref,
               s_buf, p_buf, o_buf, att_buf):
    tm = ROW_TILE
    scale = X_HEAD_DIM ** -0.5
    for h in range(X_HEADS):
        lo = h * X_HEAD_DIM
        s_buf[h] = lax.dot_general(
            q_buf[:, lo:lo + X_HEAD_DIM], k_ref[:, lo:lo + X_HEAD_DIM],
            (((1,), (1,)), ((), ())), preferred_element_type=F32)
        for r in range(0, tm, CONV_ROWS):
            s = s_buf[h, r:r + CONV_ROWS, :] * scale
            e = jnp.exp(s - jnp.max(s, axis=-1, keepdims=True))
            p = e / jnp.sum(e, axis=-1, keepdims=True)
            p_buf[h, r:r + CONV_ROWS, :] = p.astype(BF16)
        o_buf[:, lo:lo + X_HEAD_DIM] = jnp.dot(
            p_buf[h], v_ref[:, lo:lo + X_HEAD_DIM],
            preferred_element_type=F32).astype(BF16)
    att_buf[...] = jnp.dot(o_buf[...], wo_ref[...], preferred_element_type=F32)
    _residual_ln(alpha, x_ref, att_buf, g2_ref, b2_ref, o_ref, row0, tm)


def _attn_kernel(alpha, x_ref, xn_ref, wq_ref, k_ref, v_ref, wo_ref, g2_ref, b2_ref,
                 o_ref, q0, q1, s_buf, p_buf, o_buf, att_buf):
    tm = ROW_TILE
    i = pl.program_id(0)

    def project(rows_ref, dst):
        dst[...] = jnp.dot(rows_ref[...].astype(BF16), wq_ref[...],
                           preferred_element_type=F32).astype(BF16)

    @pl.when(i == 0)
    def _():
        project(x_ref.at[0:tm, :], q0)

    tile = functools.partial(
        _attn_tile, alpha, x_ref=x_ref, k_ref=k_ref, v_ref=v_ref, wo_ref=wo_ref,
        g2_ref=g2_ref, b2_ref=b2_ref, o_ref=o_ref, s_buf=s_buf, p_buf=p_buf,
        o_buf=o_buf, att_buf=att_buf)

    project(x_ref.at[tm:2 * tm, :], q1)
    tile(q_buf=q0, row0=0)
    project(xn_ref, q0)
    tile(q_buf=q1, row0=tm)


def _mlp_kernel(alpha, x_ref, w1_ref, w2_ref, g3_ref, b3_ref, o_ref, ff_buf):
    tm = ROW_TILE
    for t in range(TILES_PER_STEP):
        row0 = t * tm
        xb = x_ref[row0:row0 + tm, :].astype(BF16)
        acc = None
        for c in range(0, D_FF, D_MODEL):
            h = jnp.dot(xb, w1_ref[:, c:c + D_MODEL], preferred_element_type=F32)
            a = jnp.square(jnp.maximum(h, 0.0)).astype(BF16)
            part = jnp.dot(a, w2_ref[c:c + D_MODEL, :], preferred_element_type=F32)
            acc = part if acc is None else acc + part
        ff_buf[t] = acc
        _residual_ln(alpha, x_ref, ff_buf.at[t], g3_ref, b3_ref, o_ref, row0, tm)


def _kv_kernel(mem_ref, wkv_ref, k_ref, v_ref):
    kv = jnp.dot(mem_ref[...].astype(BF16), wkv_ref[...], preferred_element_type=F32)
    k_ref[...] = kv[:, :D_MODEL].astype(BF16)
    v_ref[...] = kv[:, D_MODEL:].astype(BF16)


def _resident(shape, index_map):
    return pl.BlockSpec(shape, index_map, pipeline_mode=pl.Buffered(1))


def _params(n_axes):
    return pltpu.CompilerParams(
        dimension_semantics=("arbitrary",) * n_axes,
        vmem_limit_bytes=VMEM_LIMIT_BYTES)


def kernel(x, mem, w_in, conv_a_w, conv_a_b, ln_a_g, ln_a_b, ln_v_g, ln_v_b, w_s, b_s,
           conv_c_w, w_out, ln1_g, ln1_b, w_q, w_kv, w_o, ln2_g, ln2_b, w_ff1, w_ff2,
           ln3_g, ln3_b):
    batch, seq, d = x.shape
    depth = w_in.shape[0]
    mem_len = mem.shape[1]
    assert d == D_MODEL and mem_len == MEM_LEN and seq % STEP_ROWS == 0
    alpha = (2.0 * depth) ** 0.25
    tm = ROW_TILE
    n_rows = batch * seq
    n_steps = n_rows // STEP_ROWS
    n_tiles = n_rows // tm
    steps_per_seq = seq // STEP_ROWS

    w_in_b, w_out_b, w_q_b, w_kv_b, w_o_b, w_ff1_b, w_ff2_b = (
        w.astype(BF16) for w in (w_in, w_out, w_q, w_kv, w_o, w_ff1, w_ff2))
    row = lambda p: p.reshape(depth, 1, p.shape[-1])
    conv_a_b, ln_a_g, ln_a_b, ln_v_g, ln_v_b = map(row, (conv_a_b, ln_a_g, ln_a_b, ln_v_g, ln_v_b))
    ln1_g, ln1_b, ln2_g, ln2_b, ln3_g, ln3_b = map(row, (ln1_g, ln1_b, ln2_g, ln2_b, ln3_g, ln3_b))
    w_s2 = w_s.reshape(depth, B_HEADS * CHUNK, CHUNK)
    b_s2 = b_s.reshape(depth, B_HEADS * CHUNK, 1)

    mem2 = mem.reshape(batch * mem_len, d)
    n_kv_tiles = (batch * mem_len) // KV_ROW_TILE
    k_all, v_all = pl.pallas_call(
        _kv_kernel,
        grid=(depth, n_kv_tiles),
        in_specs=[
            pl.BlockSpec((KV_ROW_TILE, d), lambda l, i: (i, 0)),
            pl.BlockSpec((None, d, 2 * d), lambda l, i: (l, 0, 0)),
        ],
        out_specs=[
            pl.BlockSpec((None, KV_ROW_TILE, d), lambda l, i: (l, i, 0)),
            pl.BlockSpec((None, KV_ROW_TILE, d), lambda l, i: (l, i, 0)),
        ],
        out_shape=[jax.ShapeDtypeStruct((depth, batch * mem_len, d), BF16)] * 2,
        compiler_params=_params(2),
        name="kv_proj",
    )(mem2, w_kv_b)
    k_all = k_all.reshape(depth, batch, mem_len, d)
    v_all = v_all.reshape(depth, batch, mem_len, d)

    x = x.reshape(n_rows, d)
    tile_spec = pl.BlockSpec((tm, d), lambda i: (i, 0))
    step_spec = pl.BlockSpec((STEP_ROWS, d), lambda i: (i, 0))
    next_spec = pl.BlockSpec(
        (tm, d), lambda i: (jnp.minimum(TILES_PER_STEP * (i + 1), n_tiles - 1), 0))
    x_shape = jax.ShapeDtypeStruct((n_rows, d), F32)

    for l in range(depth):
        lsel = lambda i, l=l: (l, 0, 0)
        vec = lambda n: _resident((None, 1, n), lsel)
        kv_spec = pl.BlockSpec((None, None, mem_len, d),
                               lambda i, l=l: (l, i // steps_per_seq, 0, 0))

        x = pl.pallas_call(
            functools.partial(_mixer_kernel, alpha, seq // tm),
            grid=(n_tiles,),
            in_specs=[
                tile_spec,
                _resident((None, d, IN_WIDTH), lsel),
                _resident((None, CONV_A_WIDTH, D_A), lsel),
                vec(D_A), vec(D_A), vec(D_A), vec(D_B), vec(D_B),
                _resident((None, B_HEADS * CHUNK, CHUNK), lsel),
                _resident((None, B_HEADS * CHUNK, 1), lsel),
                _resident((None, CONV_C_WIDTH, D_C), lsel),
                _resident((None, d, d), lsel),
                vec(d), vec(d),
            ],
            out_specs=tile_spec,
            out_shape=x_shape,
            scratch_shapes=[
                pltpu.VMEM((tm, IN_WIDTH), F32),
                pltpu.VMEM((D_A // LANES, HALO_A + tm, LANES), F32),
                pltpu.VMEM((D_C // LANES, HALO_C + tm, LANES), F32),
                pltpu.VMEM((tm, d), BF16),
                pltpu.VMEM((tm, d), F32),
            ],
            compiler_params=_params(1),
            name="mixer",
        )(x, w_in_b, conv_a_w, conv_a_b, ln_a_g, ln_a_b, ln_v_g, ln_v_b, w_s2, b_s2,
          conv_c_w, w_out_b, ln1_g, ln1_b)

        x = pl.pallas_call(
            functools.partial(_attn_kernel, alpha),
            grid=(n_steps,),
            in_specs=[
                step_spec, next_spec,
                _resident((None, d, d), lsel),
                kv_spec, kv_spec,
                _resident((None, d, d), lsel),
                vec(d), vec(d),
            ],
            out_specs=step_spec,
            out_shape=x_shape,
            scratch_shapes=[
                pltpu.VMEM((tm, d), BF16),
                pltpu.VMEM((tm, d), BF16),
                pltpu.VMEM((X_HEADS, tm, mem_len), F32),
                pltpu.VMEM((X_HEADS, tm, mem_len), BF16),
                pltpu.VMEM((tm, d), BF16),
                pltpu.VMEM((tm, d), F32),
            ],
            compiler_params=_params(1),
            name="cross_attn",
        )(x, x, w_q_b, k_all, v_all, w_o_b, ln2_g, ln2_b)

        x = pl.pallas_call(
            functools.partial(_mlp_kernel, alpha),
            grid=(n_steps,),
            in_specs=[
                step_spec,
                _resident((None, d, D_FF), lsel),
                _resident((None, D_FF, d), lsel),
                vec(d), vec(d),
            ],
            out_specs=step_spec,
            out_shape=x_shape,
            scratch_shapes=[pltpu.VMEM((TILES_PER_STEP, tm, d), F32)],
            compiler_params=_params(1),
            name="sq_relu_mlp",
        )(x, w_ff1_b, w_ff2_b, ln3_g, ln3_b)

    return x.reshape(batch, seq, d)
```
